```python
import math
import jax
import jax.numpy as jnp
from jax import lax
import numpy as np

D_MODEL = 1024
BATCH = 8
SEQ = 2048
DEPTH = 2
DEC_BATCH = 32
DEC_SEQ = 8
PAST_LEN = 16384
PAGE_SIZE = 128

N_EVEN = (DEPTH + 1) // 2
N_ODD = DEPTH // 2
HEAD_DIM = 64
SB_HEADS = 8
SB_WIDTH = SB_HEADS * HEAD_DIM
SB_BLOCK = 128
SSM_WIDTH = D_MODEL // 2
SSM_GROUP = 16
SSM_GROUPS = SSM_WIDTH // SSM_GROUP
SSM_STATE = 64
DIL_PATTERNS = ((128, 1), (512, 4), (2048, 16))
DIL_HEADS = 4
DIL_WIDTH = DIL_HEADS * HEAD_DIM
DIL_BLOCK = 128
ML_HEADS = 4
ML_HEAD_DIM = 128
ML_WIDTH = ML_HEADS * ML_HEAD_DIM
ML_CHUNK = 64
D_FF = 4 * D_MODEL
ROPE_THETA = 10000.0
LN_EPS = 1e-5
DN_ALPHA = (2 * DEPTH) ** 0.25
DN_BETA = (8 * DEPTH) ** -0.25
EVEN_IN = 3 * SB_WIDTH + SSM_WIDTH
EVEN_OUT = SB_WIDTH + SSM_WIDTH
ODD_IN = 3 * len(DIL_PATTERNS) * DIL_WIDTH + 4 * ML_WIDTH + 2 * ML_HEADS
ODD_OUT = DIL_WIDTH + ML_WIDTH

kernel_name = "hybrid_sb_s5_dilated_mlstm_step"


def layer_norm(x, g, b):
    xf = x.astype(jnp.float32)
    mu = jnp.mean(xf, axis=-1, keepdims=True)
    xc = xf - mu
    var = jnp.mean(xc * xc, axis=-1, keepdims=True)
    return (xc * lax.rsqrt(var + LN_EPS) * g + b).astype(x.dtype)


def rope(x, pos):
    d = x.shape[-1]
    inv_freq = ROPE_THETA ** (-jnp.arange(0, d, 2, dtype=jnp.float32) / d)
    ang = pos.astype(jnp.float32)[:, None] * inv_freq[None, :]
    cos = jnp.cos(ang)[None, :, None, :]
    sin = jnp.sin(ang)[None, :, None, :]
    xf = x.astype(jnp.float32)
    x1, x2 = xf[..., : d // 2], xf[..., d // 2:]
    return jnp.concatenate([x1 * cos - x2 * sin, x2 * cos + x1 * sin], axis=-1).astype(x.dtype)


def channel_mixer(x, w1, w2):
    return jnp.square(jax.nn.relu(x @ w1)) @ w2


def stick_breaking_attention(q, k, v, bias):
    Bsz, Lq, H, d = q.shape
    Lk = k.shape[1]
    off = Lk - Lq
    blk = SB_BLOCK if Lq % SB_BLOCK == 0 else Lq
    bias_f = bias.astype(jnp.float32)[None, :, None, None]
    outs = []
    for qs in range(0, Lq, blk):
        ke = off + qs + blk
        z = jnp.einsum("bqhd,bkhd->bhqk", q[:, qs:qs + blk], k[:, :ke]).astype(jnp.float32) / math.sqrt(d) + bias_f
        qpos = off + qs + jnp.arange(blk)
        causal = jnp.arange(ke)[None, :] < qpos[:, None]
        log_1mb = jnp.where(causal, jax.nn.log_sigmoid(-z), 0.0)
        later = lax.cumsum(log_1mb, axis=3, reverse=True) - log_1mb
        w = jnp.where(causal, jnp.exp(jax.nn.log_sigmoid(z) + later), 0.0)
        outs.append(jnp.einsum("bhqk,bkhd->bqhd", w.astype(v.dtype), v[:, :ke]))
    return jnp.concatenate(outs, axis=1)


def s5_ssm(u, lam_re, lam_im, log_dt, b_re, b_im, c_re, c_im, d_skip, w_glu, h0):
    Bsz, L, _ = u.shape
    ug = u.reshape(Bsz, L, SSM_GROUPS, SSM_GROUP).astype(jnp.float32)
    dt = jnp.exp(log_dt.astype(jnp.float32))[:, None]
    lr, li = lam_re.astype(jnp.float32), lam_im.astype(jnp.float32)
    mag = jnp.exp(lr * dt)
    ar, ai = mag * jnp.cos(li * dt), mag * jnp.sin(li * dt)
    den = lr * lr + li * li
    gr = ((ar - 1.0) * lr + ai * li) / den
    gi = (ai * lr - (ar - 1.0) * li) / den
    bbr = gr[..., None] * b_re - gi[..., None] * b_im
    bbi = gr[..., None] * b_im + gi[..., None] * b_re
    xr = jnp.einsum("blgc,gpc->blgp", ug, bbr)
    xi = jnp.einsum("blgc,gpc->blgp", ug, bbi)
    h0r, h0i = h0[:, 0].astype(jnp.float32), h0[:, 1].astype(jnp.float32)
    xr = xr.at[:, 0].add(ar * h0r - ai * h0i)
    xi = xi.at[:, 0].add(ar * h0i + ai * h0r)
    a_r = jnp.broadcast_to(ar, xr.shape)
    a_i = jnp.broadcast_to(ai, xr.shape)

    def combine(e1, e2):
        a1r, a1i, b1r, b1i = e1
        a2r, a2i, b2r, b2i = e2
        return (a2r * a1r - a2i * a1i, a2r * a1i + a2i * a1r,
                a2r * b1r - a2i * b1i + b2r, a2r * b1i + a2i * b1r + b2i)

    _, _, hr, hi = lax.associative_scan(combine, (a_r, a_i, xr, xi), axis=1)
    y = (jnp.einsum("blgp,gcp->blgc", hr, c_re) - jnp.einsum("blgp,gcp->blgc", hi, c_im) + d_skip * ug)
    y = jax.nn.gelu(y.reshape(Bsz, L, SSM_WIDTH))
    y = y * jax.nn.sigmoid(y @ w_glu)
    h_last = jnp.stack([hr[:, -1], hi[:, -1]], axis=1)
    return y.astype(u.dtype), h_last


def dilated_group_attention(q, k_ext, v_ext, q_pos0, window, dil):
    Bsz, Lq, H, d = q.shape
    n_keys = window // dil + 1
    blk = DIL_BLOCK if Lq % DIL_BLOCK == 0 else Lq
    m = jnp.arange(n_keys)

    def one_block(bi):
        i = bi * blk + jnp.arange(blk)
        idx = window + i[:, None] - dil * m[None, :]
        valid = (q_pos0 + i[:, None] - dil * m[None, :]) >= 0
        qb = lax.dynamic_slice_in_dim(q, bi * blk, blk, axis=1)
        kg = jnp.take(k_ext, idx, axis=1)
        vg = jnp.take(v_ext, idx, axis=1)
        s = jnp.einsum("bqhd,bqmhd->bhqm", qb, kg).astype(jnp.float32) / math.sqrt(d)
        s = jnp.where(valid[None, None], s, -jnp.inf)
        lse = jax.nn.logsumexp(s, axis=-1)
        p = jnp.exp(s - lse[..., None])
        return jnp.einsum("bhqm,bqmhd->bqhd", p.astype(vg.dtype), vg), lse

    o, lse = lax.map(one_block, jnp.arange(Lq // blk))
    o = jnp.moveaxis(o, 0, 1).reshape(Bsz, Lq, H, d)
    lse = jnp.moveaxis(lse, 0, 2).reshape(Bsz, H, Lq)
    return o, lse


def mlstm_chunkwise(q, k, v, i_pre, f_pre, c0, n0, m0):
    Bsz, L, H, d = q.shape
    T = math.gcd(L, ML_CHUNK)
    nc = L // T

    def to_chunks(a):
        return jnp.moveaxis(a.astype(jnp.float32).reshape((Bsz, nc, T) + a.shape[2:]), 1, 0)

    xs = (to_chunks(q), to_chunks(k) / math.sqrt(d), to_chunks(v), to_chunks(i_pre),
          to_chunks(jax.nn.log_sigmoid(f_pre.astype(jnp.float32))))
    tri = jnp.tril(jnp.ones((T, T), dtype=bool))

    def step(carry, xc):
        c, n, m = carry
        qb, kb, vb, ib, lfb = xc
        bt = jnp.moveaxis(jnp.cumsum(lfb, axis=1), 1, 2)
        it = jnp.moveaxis(ib, 1, 2)
        log_d = jnp.where(tri, bt[..., :, None] - bt[..., None, :] + it[..., None, :], -jnp.inf)
        inter = bt + m[..., None]
        m_t = jnp.maximum(inter, jnp.max(log_d, axis=-1))
        s = jnp.einsum("bthd,bshd->bhts", qb, kb) * jnp.exp(log_d - m_t[..., None])
        w_inter = jnp.exp(inter - m_t)
        num = (jnp.einsum("bhts,bshd->bthd", s, vb)
               + jnp.moveaxis(w_inter, 2, 1)[..., None] * jnp.einsum("bhvk,bthk->bthv", c, qb))
        den = jnp.sum(s, axis=-1) + w_inter * jnp.einsum("bhk,bthk->bht", n, qb)
        den = jnp.maximum(jnp.abs(den), jnp.exp(-m_t))
        h = num / jnp.moveaxis(den, 2, 1)[..., None]
        m_new = m_t[..., -1]
        w_old = jnp.exp(bt[..., -1] + m - m_new)
        w_s = jnp.exp(bt[..., -1:] - bt + it - m_new[..., None])
        c_new = w_old[..., None, None] * c + jnp.einsum("bhs,bshv,bshk->bhvk", w_s, vb, kb)
        n_new = w_old[..., None] * n + jnp.einsum("bhs,bshk->bhk", w_s, kb)
        return (c_new, n_new, m_new), h

    carry0 = (c0.astype(jnp.float32), n0.astype(jnp.float32), m0.astype(jnp.float32))
    (c, n, m), h = lax.scan(step, carry0, xs)
    return jnp.moveaxis(h, 0, 1).reshape(Bsz, L, H, d), c, n, m


def even_mixer(x, sb_past, ssm_h0, w_in, w_out, sb_bias, lam_re, lam_im, log_dt, b_re, b_im, c_re, c_im, d_skip, w_glu):
    Bsz, L, _ = x.shape
    q, k, v, u = jnp.split(x @ w_in, [SB_WIDTH, 2 * SB_WIDTH, 3 * SB_WIDTH], axis=-1)
    shp = (Bsz, L, SB_HEADS, HEAD_DIM)
    q, k, v = q.reshape(shp), k.reshape(shp), v.reshape(shp)
    new_kv = jnp.stack([k, v], axis=2)
    if sb_past is None:
        k_all, v_all = k, v
    else:
        k_all = jnp.concatenate([sb_past[:, :, 0].astype(k.dtype), k], axis=1)
        v_all = jnp.concatenate([sb_past[:, :, 1].astype(v.dtype), v], axis=1)
    a_out = stick_breaking_attention(q, k_all, v_all, sb_bias).reshape(Bsz, L, SB_WIDTH)
    b_out, h_last = s5_ssm(u, lam_re, lam_im, log_dt, b_re, b_im, c_re, c_im, d_skip, w_glu, ssm_h0)
    out = jnp.concatenate([a_out, b_out.astype(a_out.dtype)], axis=-1) @ w_out
    return out, new_kv, h_last


def odd_mixer(x, pos0, dil_bufs, c0, n0, m0, w_in, w_out, b_if):
    Bsz, L, _ = x.shape
    n_dil = len(DIL_PATTERNS)
    sizes = [DIL_WIDTH] * (3 * n_dil) + [ML_WIDTH] * 3 + [ML_HEADS] * 2 + [ML_WIDTH]
    offs = [int(o) for o in np.cumsum(sizes)[:-1]]
    parts = jnp.split(x @ w_in, offs, axis=-1)
    pos = pos0 + jnp.arange(L)
    outs, lses, rows = [], [], []
    for g, (window, dil) in enumerate(DIL_PATTERNS):
        qg, kg, vg = [p.reshape(Bsz, L, DIL_HEADS, HEAD_DIM) for p in parts[3 * g:3 * g + 3]]
        qg, kg = rope(qg, pos), rope(kg, pos)
        new_kv = jnp.stack([kg, vg], axis=2)
        if dil_bufs is None:
            prefix = jnp.zeros((Bsz, window) + new_kv.shape[2:], new_kv.dtype)
            rows.append(new_kv[:, L - min(window, L):])
        else:
            buf = dil_bufs[g]
            prefix = jnp.pad(buf, ((0, 0), (window - buf.shape[1], 0), (0, 0), (0, 0), (0, 0)))
            rows.append(new_kv)
        ext = jnp.concatenate([prefix.astype(new_kv.dtype), new_kv], axis=1)
        o, lse = dilated_group_attention(qg, ext[:, :, 0], ext[:, :, 1], pos0, window, dil)
        outs.append(o)
        lses.append(lse)
    wts = jax.nn.softmax(jnp.stack(lses), axis=0)
    c_out = jnp.einsum("gbhl,gblhd->blhd", wts.astype(x.dtype), jnp.stack(outs)).reshape(Bsz, L, DIL_WIDTH)
    mq, mk, mv, ip, fp, og = parts[3 * n_dil:]
    mshp = (Bsz, L, ML_HEADS, ML_HEAD_DIM)
    h, c, n, m = mlstm_chunkwise(mq.reshape(mshp), mk.reshape(mshp), mv.reshape(mshp),
                                 ip + b_if[0], fp + b_if[1], c0, n0, m0)
    d_out = (jax.nn.sigmoid(og.astype(jnp.float32)) * h.reshape(Bsz, L, ML_WIDTH)).astype(x.dtype)
    out = jnp.concatenate([c_out, d_out], axis=-1) @ w_out
    return out, rows, c, n, m


def setup_inputs(seed: int = 0) -> dict:
    key = jax.random.key(seed)
    keys = iter(jax.random.split(key, 48))
    f32 = jnp.float32

    def normal(shape, scale=1.0):
        return scale * jax.random.normal(next(keys), shape, f32)

    n_pages = PAST_LEN // PAGE_SIZE
    n_used = DEC_BATCH * n_pages
    n_pool = n_used + max(n_used // 4, 1)
    perm = jax.random.permutation(next(keys), n_pool)
    page_table = perm[:n_used].reshape(DEC_BATCH, n_pages).astype(jnp.int32)
    inp = {}
    inp["x_prompt"] = normal((BATCH, SEQ, D_MODEL))
    inp["x_sample"] = normal((DEC_BATCH, DEC_SEQ, D_MODEL))
    inp["cache_sb_kv"] = normal((N_EVEN, n_pool, PAGE_SIZE, 2, SB_HEADS, HEAD_DIM))
    inp["state_ssm"] = normal((N_EVEN, DEC_BATCH, 2, SSM_GROUPS, SSM_STATE), 0.1)
    for g, (window, _) in enumerate(DIL_PATTERNS):
        inp["cache_dil%d_kv" % g] = normal((N_ODD, DEC_BATCH, min(window, PAST_LEN), 2, DIL_HEADS, HEAD_DIM))
    inp["state_mlstm_c"] = normal((N_ODD, DEC_BATCH, ML_HEADS, ML_HEAD_DIM, ML_HEAD_DIM), 0.05)
    inp["state_mlstm_n"] = normal((N_ODD, DEC_BATCH, ML_HEADS, ML_HEAD_DIM), 0.1)
    inp["state_mlstm_m"] = normal((N_ODD, DEC_BATCH, ML_HEADS), 0.5)
    inp["page_table"] = page_table
    inp["w_in_even"] = normal((N_EVEN, D_MODEL, EVEN_IN), D_MODEL ** -0.5)
    inp["w_out_even"] = normal((N_EVEN, EVEN_OUT, D_MODEL), DN_BETA * EVEN_OUT ** -0.5)
    inp["sb_bias"] = jnp.broadcast_to(jnp.linspace(-9.0, -5.0, SB_HEADS, dtype=f32), (N_EVEN, SB_HEADS)) + normal((N_EVEN, SB_HEADS), 0.1)
    inp["ssm_lambda_re"] = -0.5 + normal((N_EVEN, SSM_GROUPS, SSM_STATE), 0.01)
    inp["ssm_lambda_im"] = math.pi * jnp.arange(SSM_STATE, dtype=f32) + normal((N_EVEN, SSM_GROUPS, SSM_STATE), 0.01)
    inp["ssm_log_dt"] = jax.random.uniform(next(keys), (N_EVEN, SSM_GROUPS), f32, math.log(1e-3), math.log(1e-1))
    inp["ssm_b_re"] = normal((N_EVEN, SSM_GROUPS, SSM_STATE, SSM_GROUP), (2 * SSM_GROUP) ** -0.5)
    inp["ssm_b_im"] = normal((N_EVEN, SSM_GROUPS, SSM_STATE, SSM_GROUP), (2 * SSM_GROUP) ** -0.5)
    inp["ssm_c_re"] = normal((N_EVEN, SSM_GROUPS, SSM_GROUP, SSM_STATE), SSM_STATE ** -0.5)
    inp["ssm_c_im"] = normal((N_EVEN, SSM_GROUPS, SSM_GROUP, SSM_STATE), SSM_STATE ** -0.5)
    inp["ssm_d"] = normal((N_EVEN, SSM_GROUPS, SSM_GROUP))
    inp["ssm_w_glu"] = normal((N_EVEN, SSM_WIDTH, SSM_WIDTH), SSM_WIDTH ** -0.5)
    inp["w_in_odd"] = normal((N_ODD, D_MODEL, ODD_IN), D_MODEL ** -0.5)
    inp["w_out_odd"] = normal((N_ODD, ODD_OUT, D_MODEL), DN_BETA * ODD_OUT ** -0.5)
    inp["mlstm_b_if"] = jnp.stack([normal((N_ODD, ML_HEADS), 0.1),
                                   jnp.linspace(3.0, 6.0, ML_HEADS, dtype=f32) + normal((N_ODD, ML_HEADS), 0.1)], axis=1)
    inp["ln_mix_g"] = 1.0 + normal((DEPTH, D_MODEL), 0.02)
    inp["ln_mix_b"] = normal((DEPTH, D_MODEL), 0.02)
    inp["ln_ffn_g"] = 1.0 + normal((DEPTH, D_MODEL), 0.02)
    inp["ln_ffn_b"] = normal((DEPTH, D_MODEL), 0.02)
    inp["ffn_w1"] = normal((DEPTH, D_MODEL, D_FF), D_MODEL ** -0.5)
    inp["ffn_w2"] = normal((DEPTH, D_FF, D_MODEL), DN_BETA * D_FF ** -0.5)
    return inp


def reference(x_prompt, x_sample, cache_sb_kv, state_ssm, cache_dil0_kv, cache_dil1_kv, cache_dil2_kv,
              state_mlstm_c, state_mlstm_n, state_mlstm_m, page_table, w_in_even, w_out_even, sb_bias,
              ssm_lambda_re, ssm_lambda_im, ssm_log_dt, ssm_b_re, ssm_b_im, ssm_c_re, ssm_c_im, ssm_d, ssm_w_glu,
              w_in_odd, w_out_odd, mlstm_b_if, ln_mix_g, ln_mix_b, ln_ffn_g, ln_ffn_b, ffn_w1, ffn_w2):
    Bp = x_prompt.shape[0]
    Bs = x_sample.shape[0]
    xp, xs = x_prompt, x_sample
    p_sb, s_sb, p_ssm, s_ssm = [], [], [], []
    p_dil, s_dil = [[], [], []], [[], [], []]
    p_c, s_c, p_n, s_n, p_m, s_m = [], [], [], [], [], []
    for layer in range(DEPTH):
        if layer % 2 == 0:
            e = layer // 2
            ssm_params = (ssm_lambda_re[e], ssm_lambda_im[e], ssm_log_dt[e], ssm_b_re[e], ssm_b_im[e],
                          ssm_c_re[e], ssm_c_im[e], ssm_d[e], ssm_w_glu[e])
            sb_past = cache_sb_kv[e][page_table].reshape(Bs, -1, 2, SB_HEADS, HEAD_DIM)
            h0p = jnp.zeros((Bp, 2, SSM_GROUPS, SSM_STATE), jnp.float32)
            mp, kvp, hp = even_mixer(xp, None, h0p, w_in_even[e], w_out_even[e], sb_bias[e], *ssm_params)
            ms, kvs, hs = even_mixer(xs, sb_past, state_ssm[e], w_in_even[e], w_out_even[e], sb_bias[e], *ssm_params)
            p_sb.append(kvp)
            s_sb.append(kvs)
            p_ssm.append(hp)
            s_ssm.append(hs)
        else:
            o = layer // 2
            c0 = jnp.zeros((Bp, ML_HEADS, ML_HEAD_DIM, ML_HEAD_DIM), jnp.float32)
            n0 = jnp.zeros((Bp, ML_HEADS, ML_HEAD_DIM), jnp.float32)
            m0 = jnp.zeros((Bp, ML_HEADS), jnp.float32)
            mp, rows_p, cp, np_state, mp_state = odd_mixer(xp, 0, None, c0, n0, m0,
                                                            w_in_odd[o], w_out_odd[o], mlstm_b_if[o])
            bufs = (cache_dil0_kv[o], cache_dil1_kv[o], cache_dil2_kv[o])
            ms, rows_s, cs, ns_state, ms_state = odd_mixer(xs, PAST_LEN, bufs, state_mlstm_c[o], state_mlstm_n[o],
                                                            state_mlstm_m[o], w_in_odd[o], w_out_odd[o], mlstm_b_if[o])
            for g in range(len(DIL_PATTERNS)):
                p_dil[g].append(rows_p[g])
                s_dil[g].append(rows_s[g])
            p_c.append(cp)
            s_c.append(cs)
            p_n.append(np_state)
            s_n.append(ns_state)
            p_m.append(mp_state)
            s_m.append(ms_state)
        xp = layer_norm(DN_ALPHA * xp + mp, ln_mix_g[layer], ln_mix_b[layer])
        xp = layer_norm(DN_ALPHA * xp + channel_mixer(xp, ffn_w1[layer], ffn_w2[layer]), ln_ffn_g[layer], ln_ffn_b[layer])
        xs = layer_norm(DN_ALPHA * xs + ms, ln_mix_g[layer], ln_mix_b[layer])
        xs = layer_norm(DN_ALPHA * xs + channel_mixer(xs, ffn_w1[layer], ffn_w2[layer]), ln_ffn_g[layer], ln_ffn_b[layer])
    sb_kv_prompt, sb_kv_sample = jnp.stack(p_sb), jnp.stack(s_sb)
    ssm_prompt, ssm_sample = jnp.stack(p_ssm), jnp.stack(s_ssm)
    dil0_prompt, dil0_sample = jnp.stack(p_dil[0]), jnp.stack(s_dil[0])
    dil1_prompt, dil1_sample = jnp.stack(p_dil[1]), jnp.stack(s_dil[1])
    dil2_prompt, dil2_sample = jnp.stack(p_dil[2]), jnp.stack(s_dil[2])
    mlstm_c_prompt, mlstm_c_sample = jnp.stack(p_c), jnp.stack(s_c)
    mlstm_n_prompt, mlstm_n_sample = jnp.stack(p_n), jnp.stack(s_n)
    mlstm_m_prompt, mlstm_m_sample = jnp.stack(p_m), jnp.stack(s_m)
    return (xp, xs, sb_kv_prompt, sb_kv_sample, ssm_prompt, ssm_sample,
            dil0_prompt, dil0_sample, dil1_prompt, dil1_sample, dil2_prompt, dil2_sample,
            mlstm_c_prompt, mlstm_c_sample, mlstm_n_prompt, mlstm_n_sample, mlstm_m_prompt, mlstm_m_sample)
```

```python
import functools
import math

import jax
import jax.numpy as jnp
import numpy as np
from jax import lax
from jax.experimental import pallas as pl
from jax.experimental.pallas import tpu as pltpu

D_MODEL = 1024
DEPTH = 2
PAST_LEN = 16384
PAGE_SIZE = 128
HEAD_DIM = 64
SB_HEADS = 8
SB_WIDTH = SB_HEADS * HEAD_DIM
SB_BLOCK = 128
SSM_WIDTH = D_MODEL // 2
SSM_GROUP = 16
SSM_GROUPS = SSM_WIDTH // SSM_GROUP
SSM_STATE = 64
DIL_PATTERNS = ((128, 1), (512, 4), (2048, 16))
DIL_HEADS = 4
DIL_WIDTH = DIL_HEADS * HEAD_DIM
DIL_BLOCK = 128
ML_HEADS = 4
ML_HEAD_DIM = 128
ML_WIDTH = ML_HEADS * ML_HEAD_DIM
ML_CHUNK = 64
D_FF = 4 * D_MODEL
ROPE_THETA = 10000.0
LN_EPS = 1e-5
DN_ALPHA = (2 * DEPTH) ** 0.25

V7X_LANES = 128
V7X_VMEM_LIMIT_BYTES = 56 * 1024 * 1024

_BF16 = jnp.bfloat16
_F32 = jnp.float32


def _row_tile(m, target):
    t = min(m, target)
    while m % t:
        t //= 2
    return t


def _params(n_axes=1):
    return pltpu.CompilerParams(dimension_semantics=("arbitrary",) * n_axes,
                                vmem_limit_bytes=V7X_VMEM_LIMIT_BYTES)


def _resident(shape):
    return pl.BlockSpec(shape, lambda i: (0,) * len(shape), pipeline_mode=pl.Buffered(1))


def _proj_body(x_ref, w_ref, *out_refs, bounds):
    xb = x_ref[...].astype(_BF16)
    for o_ref, (lo, hi) in zip(out_refs, bounds):
        o_ref[...] = jnp.dot(xb, w_ref[:, lo:hi], preferred_element_type=_F32)


def project(x, w_bf16, widths, row_tile=512):
    m, k = x.shape
    tm = _row_tile(m, row_tile)
    offs = np.concatenate([[0], np.cumsum(widths)])
    bounds = tuple((int(offs[i]), int(offs[i + 1])) for i in range(len(widths)))
    assert all(lo % V7X_LANES == 0 for lo, _ in bounds)
    return pl.pallas_call(
        functools.partial(_proj_body, bounds=bounds),
        grid=(m // tm,),
        in_specs=[pl.BlockSpec((tm, k), lambda i: (i, 0)), _resident(w_bf16.shape)],
        out_specs=[pl.BlockSpec((tm, wd), lambda i: (i, 0)) for wd in widths],
        out_shape=[jax.ShapeDtypeStruct((m, wd), _F32) for wd in widths],
        compiler_params=_params(),
        name="in_proj",
    )(x, w_bf16)


def _layer_norm_rows(y, g, b):
    mu = jnp.mean(y, axis=-1, keepdims=True)
    yc = y - mu
    var = jnp.mean(yc * yc, axis=-1, keepdims=True)
    return yc * lax.rsqrt(var + LN_EPS) * g + b


def _out_proj_body(*refs, n_parts):
    a_refs = refs[:n_parts]
    w_refs = refs[n_parts:2 * n_parts]
    x_ref, g_ref, b_ref, o_ref = refs[2 * n_parts:]
    acc = DN_ALPHA * x_ref[...]
    for a_ref, w_ref in zip(a_refs, w_refs):
        acc = acc + jnp.dot(a_ref[...].astype(_BF16), w_ref[...], preferred_element_type=_F32)
    o_ref[...] = _layer_norm_rows(acc, g_ref[...], b_ref[...])


def out_proj_norm(parts, w_parts, x, g, b, row_tile=512):
    m, d = x.shape
    tm = _row_tile(m, row_tile)
    n = len(parts)
    in_specs = ([pl.BlockSpec((tm, p.shape[1]), lambda i: (i, 0)) for p in parts]
                + [_resident(w.shape) for w in w_parts]
                + [pl.BlockSpec((tm, d), lambda i: (i, 0)), _resident((1, d)), _resident((1, d))])
    return pl.pallas_call(
        functools.partial(_out_proj_body, n_parts=n),
        grid=(m // tm,),
        in_specs=in_specs,
        out_specs=pl.BlockSpec((tm, d), lambda i: (i, 0)),
        out_shape=jax.ShapeDtypeStruct((m, d), _F32),
        compiler_params=_params(),
        name="out_proj_norm",
    )(*parts, *w_parts, x, g.reshape(1, d), b.reshape(1, d))


def _ffn_body(x_ref, w1_ref, w2_ref, g_ref, b_ref, o_ref, *, ff_chunk):
    x = x_ref[...]
    xb = x.astype(_BF16)
    acc = DN_ALPHA * x
    for c in range(0, w1_ref.shape[1], ff_chunk):
        h = jnp.dot(xb, w1_ref[:, c:c + ff_chunk], preferred_element_type=_F32)
        h = jnp.square(jnp.maximum(h, 0.0)).astype(_BF16)
        acc = acc + jnp.dot(h, w2_ref[c:c + ff_chunk, :], preferred_element_type=_F32)
    o_ref[...] = _layer_norm_rows(acc, g_ref[...], b_ref[...])


def ffn_norm(x, w1_bf16, w2_bf16, g, b, row_tile=512, ff_chunk=1024):
    m, d = x.shape
    tm = _row_tile(m, row_tile)
    return pl.pallas_call(
        functools.partial(_ffn_body, ff_chunk=ff_chunk),
        grid=(m // tm,),
        in_specs=[pl.BlockSpec((tm, d), lambda i: (i, 0)), _resident(w1_bf16.shape), _resident(w2_bf16.shape),
                  _resident((1, d)), _resident((1, d))],
        out_specs=pl.BlockSpec((tm, d), lambda i: (i, 0)),
        out_shape=jax.ShapeDtypeStruct((m, d), _F32),
        compiler_params=_params(),
        name="ffn_norm",
    )(x, w1_bf16, w2_bf16, g.reshape(1, d), b.reshape(1, d))


def _rope(x, pos):
    d = x.shape[-1]
    inv_freq = ROPE_THETA ** (-jnp.arange(0, d, 2, dtype=jnp.float32) / d)
    ang = pos.astype(jnp.float32)[:, None] * inv_freq[None, :]
    cos = jnp.cos(ang)[None, :, None, :]
    sin = jnp.sin(ang)[None, :, None, :]
    x1, x2 = x[..., : d // 2], x[..., d // 2:]
    return jnp.concatenate([x1 * cos - x2 * sin, x2 * cos + x1 * sin], axis=-1)


def _stick_breaking_attention(q, k, v, bias):
    Bsz, Lq, H, d = q.shape
    Lk = k.shape[1]
    off = Lk - Lq
    blk = SB_BLOCK if Lq % SB_BLOCK == 0 else Lq
    bias_f = bias.astype(jnp.float32)[None, :, None, None]
    outs = []
    for qs in range(0, Lq, blk):
        ke = off + qs + blk
        z = jnp.einsum("bqhd,bkhd->bhqk", q[:, qs:qs + blk], k[:, :ke]).astype(jnp.float32) / math.sqrt(d) + bias_f
        qpos = off + qs + jnp.arange(blk)
        causal = jnp.arange(ke)[None, :] < qpos[:, None]
        log_1mb = jnp.where(causal, jax.nn.log_sigmoid(-z), 0.0)
        later = lax.cumsum(log_1mb, axis=3, reverse=True) - log_1mb
        w = jnp.where(causal, jnp.exp(jax.nn.log_sigmoid(z) + later), 0.0)
        outs.append(jnp.einsum("bhqk,bkhd->bqhd", w, v[:, :ke]))
    return jnp.concatenate(outs, axis=1)


def _s5_ssm(u, lam_re, lam_im, log_dt, b_re, b_im, c_re, c_im, d_skip, w_glu, h0):
    Bsz, L, _ = u.shape
    ug = u.reshape(Bsz, L, SSM_GROUPS, SSM_GROUP)
    dt = jnp.exp(log_dt)[:, None]
    lr, li = lam_re, lam_im
    mag = jnp.exp(lr * dt)
    ar, ai = mag * jnp.cos(li * dt), mag * jnp.sin(li * dt)
    den = lr * lr + li * li
    gr = ((ar - 1.0) * lr + ai * li) / den
    gi = (ai * lr - (ar - 1.0) * li) / den
    bbr = gr[..., None] * b_re - gi[..., None] * b_im
    bbi = gr[..., None] * b_im + gi[..., None] * b_re
    xr = jnp.einsum("blgc,gpc->blgp", ug, bbr)
    xi = jnp.einsum("blgc,gpc->blgp", ug, bbi)
    h0r, h0i = h0[:, 0], h0[:, 1]
    xr = xr.at[:, 0].add(ar * h0r - ai * h0i)
    xi = xi.at[:, 0].add(ar * h0i + ai * h0r)
    a_r = jnp.broadcast_to(ar, xr.shape)
    a_i = jnp.broadcast_to(ai, xr.shape)

    def combine(e1, e2):
        a1r, a1i, b1r, b1i = e1
        a2r, a2i, b2r, b2i = e2
        return (a2r * a1r - a2i * a1i, a2r * a1i + a2i * a1r,
                a2r * b1r - a2i * b1i + b2r, a2r * b1i + a2i * b1r + b2i)

    _, _, hr, hi = lax.associative_scan(combine, (a_r, a_i, xr, xi), axis=1)
    y = (jnp.einsum("blgp,gcp->blgc", hr, c_re) - jnp.einsum("blgp,gcp->blgc", hi, c_im) + d_skip * ug)
    y = jax.nn.gelu(y.reshape(Bsz, L, SSM_WIDTH))
    y = y * jax.nn.sigmoid(y @ w_glu)
    h_last = jnp.stack([hr[:, -1], hi[:, -1]], axis=1)
    return y, h_last


def _dilated_group_attention(q, k_ext, v_ext, q_pos0, window, dil):
    Bsz, Lq, H, d = q.shape
    n_keys = window // dil + 1
    blk = DIL_BLOCK if Lq % DIL_BLOCK == 0 else Lq
    m = jnp.arange(n_keys)

    def one_block(bi):
        i = bi * blk + jnp.arange(blk)
        idx = window + i[:, None] - dil * m[None, :]
        valid = (q_pos0 + i[:, None] - dil * m[None, :]) >= 0
        qb = lax.dynamic_slice_in_dim(q, bi * blk, blk, axis=1)
        kg = jnp.take(k_ext, idx, axis=1)
        vg = jnp.take(v_ext, idx, axis=1)
        s = jnp.einsum("bqhd,bqmhd->bhqm", qb, kg).astype(jnp.float32) / math.sqrt(d)
        s = jnp.where(valid[None, None], s, -jnp.inf)
        lse = jax.nn.logsumexp(s, axis=-1)
        p = jnp.exp(s - lse[..., None])
        return jnp.einsum("bhqm,bqmhd->bqhd", p, vg), lse

    o, lse = lax.map(one_block, jnp.arange(Lq // blk))
    o = jnp.moveaxis(o, 0, 1).reshape(Bsz, Lq, H, d)
    lse = jnp.moveaxis(lse, 0, 2).reshape(Bsz, H, Lq)
    return o, lse


def _mlstm_chunkwise(q, k, v, i_pre, f_pre, c0, n0, m0):
    Bsz, L, H, d = q.shape
    T = math.gcd(L, ML_CHUNK)
    nc = L // T

    def to_chunks(a):
        return jnp.moveaxis(a.reshape((Bsz, nc, T) + a.shape[2:]), 1, 0)

    xs = (to_chunks(q), to_chunks(k) / math.sqrt(d), to_chunks(v), to_chunks(i_pre),
          to_chunks(jax.nn.log_sigmoid(f_pre)))
    tri = jnp.tril(jnp.ones((T, T), dtype=bool))

    def step(carry, xc):
        c, n, m = carry
        qb, kb, vb, ib, lfb = xc
        bt = jnp.moveaxis(jnp.cumsum(lfb, axis=1), 1, 2)
        it = jnp.moveaxis(ib, 1, 2)
        log_d = jnp.where(tri, bt[..., :, None] - bt[..., None, :] + it[..., None, :], -jnp.inf)
        inter = bt + m[..., None]
        m_t = jnp.maximum(inter, jnp.max(log_d, axis=-1))
        s = jnp.einsum("bthd,bshd->bhts", qb, kb) * jnp.exp(log_d - m_t[..., None])
        w_inter = jnp.exp(inter - m_t)
        num = (jnp.einsum("bhts,bshd->bthd", s, vb)
               + jnp.moveaxis(w_inter, 2, 1)[..., None] * jnp.einsum("bhvk,bthk->bthv", c, qb))
        den = jnp.sum(s, axis=-1) + w_inter * jnp.einsum("bhk,bthk->bht", n, qb)
        den = jnp.maximum(jnp.abs(den), jnp.exp(-m_t))
        h = num / jnp.moveaxis(den, 2, 1)[..., None]
        m_new = m_t[..., -1]
        w_old = jnp.exp(bt[..., -1] + m - m_new)
        w_s = jnp.exp(bt[..., -1:] - bt + it - m_new[..., None])
        c_new = w_old[..., None, None] * c + jnp.einsum("bhs,bshv,bshk->bhvk", w_s, vb, kb)
        n_new = w_old[..., None] * n + jnp.einsum("bhs,bshk->bhk", w_s, kb)
        return (c_new, n_new, m_new), h

    (c, n, m), h = lax.scan(step, (c0, n0, m0), xs)
    return jnp.moveaxis(h, 0, 1).reshape(Bsz, L, H, d), c, n, m


def _even_mixer(x, sb_past, ssm_h0, w_in_b, w_out_b, sb_bias, ssm_params):
    Bsz, L, _ = x.shape
    q, kv, u = project(x.reshape(Bsz * L, D_MODEL), w_in_b, (SB_WIDTH, 2 * SB_WIDTH, SSM_WIDTH))
    shp = (Bsz, L, SB_HEADS, HEAD_DIM)
    new_kv = kv.reshape(Bsz, L, 2, SB_HEADS, HEAD_DIM)
    q = q.reshape(shp)
    k, v = new_kv[:, :, 0], new_kv[:, :, 1]
    if sb_past is None:
        k_all, v_all = k, v
    else:
        k_all = jnp.concatenate([sb_past[:, :, 0], k], axis=1)
        v_all = jnp.concatenate([sb_past[:, :, 1], v], axis=1)
    a_out = _stick_breaking_attention(q, k_all, v_all, sb_bias).reshape(Bsz * L, SB_WIDTH)
    b_out, h_last = _s5_ssm(u.reshape(Bsz, L, SSM_WIDTH), *ssm_params, ssm_h0)
    parts = (a_out, b_out.reshape(Bsz * L, SSM_WIDTH))
    w_parts = (w_out_b[:SB_WIDTH], w_out_b[SB_WIDTH:])
    return parts, w_parts, new_kv, h_last


def _odd_mixer(x, pos0, dil_bufs, c0, n0, m0, w_in_b, w_out_b, b_if):
    Bsz, L, _ = x.shape
    n_dil = len(DIL_PATTERNS)
    widths = (DIL_WIDTH,) * (3 * n_dil) + (ML_WIDTH,) * 4 + (V7X_LANES,)
    parts = project(x.reshape(Bsz * L, D_MODEL), w_in_b, widths)
    pos = pos0 + jnp.arange(L)
    outs, lses, rows = [], [], []
    for g, (window, dil) in enumerate(DIL_PATTERNS):
        qg, kg, vg = [p.reshape(Bsz, L, DIL_HEADS, HEAD_DIM) for p in parts[3 * g:3 * g + 3]]
        qg, kg = _rope(qg, pos), _rope(kg, pos)
        new_kv = jnp.stack([kg, vg], axis=2)
        if dil_bufs is None:
            prefix = jnp.zeros((Bsz, window) + new_kv.shape[2:], new_kv.dtype)
            rows.append(new_kv[:, L - min(window, L):])
        else:
            buf = dil_bufs[g]
            prefix = jnp.pad(buf, ((0, 0), (window - buf.shape[1], 0), (0, 0), (0, 0), (0, 0)))
            rows.append(new_kv)
        ext = jnp.concatenate([prefix, new_kv], axis=1)
        o, lse = _dilated_group_attention(qg, ext[:, :, 0], ext[:, :, 1], pos0, window, dil)
        outs.append(o)
        lses.append(lse)
    wts = jax.nn.softmax(jnp.stack(lses), axis=0)
    c_out = jnp.einsum("gbhl,gblhd->blhd", wts, jnp.stack(outs)).reshape(Bsz * L, DIL_WIDTH)
    mq, mk, mv, og, gates = parts[3 * n_dil:]
    ip = gates[:, :ML_HEADS].reshape(Bsz, L, ML_HEADS)
    fp = gates[:, ML_HEADS:2 * ML_HEADS].reshape(Bsz, L, ML_HEADS)
    mshp = (Bsz, L, ML_HEADS, ML_HEAD_DIM)
    h, c, n, m = _mlstm_chunkwise(mq.reshape(mshp), mk.reshape(mshp), mv.reshape(mshp),
                                  ip + b_if[0], fp + b_if[1], c0, n0, m0)
    d_out = jax.nn.sigmoid(og) * h.reshape(Bsz * L, ML_WIDTH)
    w_parts = (w_out_b[:DIL_WIDTH], w_out_b[DIL_WIDTH:])
    return (c_out, d_out), w_parts, rows, c, n, m


def _odd_in_weight(w):
    n_qkv = 3 * len(DIL_PATTERNS) * DIL_WIDTH + 3 * ML_WIDTH
    gates = w[:, n_qkv:n_qkv + 2 * ML_HEADS]
    og = w[:, n_qkv + 2 * ML_HEADS:]
    gates = jnp.pad(gates, ((0, 0), (0, V7X_LANES - 2 * ML_HEADS)))
    return jnp.concatenate([w[:, :n_qkv], og, gates], axis=1).astype(_BF16)


def kernel(x_prompt, x_sample, cache_sb_kv, state_ssm, cache_dil0_kv, cache_dil1_kv, cache_dil2_kv, state_mlstm_c, state_mlstm_n, state_mlstm_m, page_table, w_in_even, w_out_even, sb_bias, ssm_lambda_re, ssm_lambda_im, ssm_log_dt, ssm_b_re, ssm_b_im, ssm_c_re, ssm_c_im, ssm_d, ssm_w_glu, w_in_odd, w_out_odd, mlstm_b_if, ln_mix_g, ln_mix_b, ln_ffn_g, ln_ffn_b, ffn_w1, ffn_w2):
    Bp, Lp, _ = x_prompt.shape
    Bs, Ls, _ = x_sample.shape
    xp = x_prompt.reshape(Bp * Lp, D_MODEL)
    xs = x_sample.reshape(Bs * Ls, D_MODEL)
    p_sb, s_sb, p_ssm, s_ssm = [], [], [], []
    p_dil, s_dil = [[], [], []], [[], [], []]
    p_c, s_c, p_n, s_n, p_m, s_m = [], [], [], [], [], []
    for layer in range(DEPTH):
        if layer % 2 == 0:
            e = layer // 2
            ssm_params = (ssm_lambda_re[e], ssm_lambda_im[e], ssm_log_dt[e], ssm_b_re[e], ssm_b_im[e],
                          ssm_c_re[e], ssm_c_im[e], ssm_d[e], ssm_w_glu[e])
            w_in_b = w_in_even[e].astype(_BF16)
            w_out_b = w_out_even[e].astype(_BF16)
            sb_past = cache_sb_kv[e][page_table].reshape(Bs, -1, 2, SB_HEADS, HEAD_DIM)
            h0p = jnp.zeros((Bp, 2, SSM_GROUPS, SSM_STATE), jnp.float32)
            pp, wp, kvp, hp = _even_mixer(xp.reshape(Bp, Lp, D_MODEL), None, h0p, w_in_b, w_out_b, sb_bias[e], ssm_params)
            ps, ws, kvs, hs = _even_mixer(xs.reshape(Bs, Ls, D_MODEL), sb_past, state_ssm[e], w_in_b, w_out_b, sb_bias[e], ssm_params)
            p_sb.append(kvp)
            s_sb.append(kvs)
            p_ssm.append(hp)
            s_ssm.append(hs)
        else:
            o = layer // 2
            w_in_b = _odd_in_weight(w_in_odd[o])
            w_out_b = w_out_odd[o].astype(_BF16)
            c0 = jnp.zeros((Bp, ML_HEADS, ML_HEAD_DIM, ML_HEAD_DIM), jnp.float32)
            n0 = jnp.zeros((Bp, ML_HEADS, ML_HEAD_DIM), jnp.float32)
            m0 = jnp.zeros((Bp, ML_HEADS), jnp.float32)
            pp, wp, rows_p, cp, np_state, mp_state = _odd_mixer(xp.reshape(Bp, Lp, D_MODEL), 0, None, c0, n0, m0,
                                                                w_in_b, w_out_b, mlstm_b_if[o])
            bufs = (cache_dil0_kv[o], cache_dil1_kv[o], cache_dil2_kv[o])
            ps, ws, rows_s, cs, ns_state, ms_state = _odd_mixer(xs.reshape(Bs, Ls, D_MODEL), PAST_LEN, bufs,
                                                                state_mlstm_c[o], state_mlstm_n[o], state_mlstm_m[o],
                                                                w_in_b, w_out_b, mlstm_b_if[o])
            for g in range(len(DIL_PATTERNS)):
                p_dil[g].append(rows_p[g])
                s_dil[g].append(rows_s[g])
            p_c.append(cp)
            s_c.append(cs)
            p_n.append(np_state)
            s_n.append(ns_state)
            p_m.append(mp_state)
            s_m.append(ms_state)
        w1_b = ffn_w1[layer].astype(_BF16)
        w2_b = ffn_w2[layer].astype(_BF16)
        xp = out_proj_norm(pp, wp, xp, ln_mix_g[layer], ln_mix_b[layer])
        xp = ffn_norm(xp, w1_b, w2_b, ln_ffn_g[layer], ln_ffn_b[layer])
        xs = out_proj_norm(ps, ws, xs, ln_mix_g[layer], ln_mix_b[layer])
        xs = ffn_norm(xs, w1_b, w2_b, ln_ffn_g[layer], ln_ffn_b[layer])
    st = jnp.stack
    return (xp.reshape(Bp, Lp, D_MODEL), xs.reshape(Bs, Ls, D_MODEL), st(p_sb), st(s_sb), st(p_ssm), st(s_ssm),
            st(p_dil[0]), st(s_dil[0]), st(p_dil[1]), st(s_dil[1]), st(p_dil[2]), st(s_dil[2]),
            st(p_c), st(s_c), st(p_n), st(s_n), st(p_m), st(s_m))
```
